```python
import jax, jax.numpy as jnp
from jax import lax
import numpy as np

D_MODEL = 1024
BATCH = 2
SEQ = 8192
DEPTH = 4
DEC_BATCH = 128
DEC_SEQ = 4
PAST_LEN = 8192
PAGE_SIZE = 128

D_MIX = D_MODEL
CONV_CH = D_MIX // 4
CONV_W = 3
POOL_CH = D_MIX // 4
POOL_GROUPS = 4
POOL_GROUP_CH = POOL_CH // POOL_GROUPS
POOL_WINDOWS = (2, 4, 8, 16)
POOL_BUF = 15
HEAD_DIM = 64
N_HEADS = (D_MIX // 2) // HEAD_DIM
KV_HEADS = 2
GROUP = N_HEADS // KV_HEADS
ATTN_CH = N_HEADS * HEAD_DIM
KV_CH = KV_HEADS * HEAD_DIM
WINDOW = 128
ROPE_THETA = 10000.0
PEER_HEADS = 8
N_KEYS = 128
N_EXPERTS = N_KEYS * N_KEYS
PEER_TOPK = 16
PEER_DK = 256
PEER_DK_HALF = PEER_DK // 2
PEER_CHUNK = 256
PLE_DIM = 256
EPS = 1e-6
NEG_INF = -1e30
D_IN = 3 * CONV_CH + POOL_CH + ATTN_CH + 2 * KV_CH
SPLIT_POINTS = (CONV_CH, 2 * CONV_CH, 3 * CONV_CH, 3 * CONV_CH + POOL_CH,
                3 * CONV_CH + POOL_CH + ATTN_CH, 3 * CONV_CH + POOL_CH + ATTN_CH + KV_CH)

kernel_name = 'hybrid_conv_pool_swa_peer_step'


def rmsnorm(x, g):
    x32 = x.astype(jnp.float32)
    y = x32 * lax.rsqrt(jnp.mean(x32 * x32, axis=-1, keepdims=True) + EPS)
    return (y * g).astype(x.dtype)


def rms_normalize(x):
    x32 = x.astype(jnp.float32)
    return (x32 * lax.rsqrt(jnp.mean(x32 * x32, axis=-1, keepdims=True) + EPS)).astype(x.dtype)


def rope(x, pos):
    half = HEAD_DIM // 2
    inv = ROPE_THETA ** (-jnp.arange(half, dtype=jnp.float32) / half)
    ang = pos.astype(jnp.float32)[:, None] * inv[None, :]
    cos = jnp.cos(ang)[None, :, None, :]
    sin = jnp.sin(ang)[None, :, None, :]
    x1 = x[..., :half].astype(jnp.float32)
    x2 = x[..., half:].astype(jnp.float32)
    return jnp.concatenate([x1 * cos - x2 * sin, x2 * cos + x1 * sin], axis=-1).astype(x.dtype)


def short_conv(u, buf, w):
    t = u.shape[1]
    up = jnp.concatenate([buf, u], axis=1)
    y = up[:, 0:t] * w[0]
    for j in range(1, CONV_W):
        y = y + up[:, j:j + t] * w[j]
    return y, up[:, -(CONV_W - 1):]


def multiscale_pool(z, buf, pos):
    t = z.shape[1]
    zp = jnp.concatenate([buf, z], axis=1)
    cs = jnp.cumsum(zp.astype(jnp.float32), axis=1)
    cs = jnp.concatenate([jnp.zeros_like(cs[:, :1]), cs], axis=1)
    top = POOL_BUF + 1
    means = []
    for g, win in enumerate(POOL_WINDOWS):
        sl = slice(g * POOL_GROUP_CH, (g + 1) * POOL_GROUP_CH)
        wsum = cs[:, top:top + t, sl] - cs[:, top - win:top - win + t, sl]
        cnt = jnp.minimum(win, pos + 1).astype(jnp.float32)
        means.append(wsum / cnt[None, :, None])
    mean = jnp.concatenate(means, axis=-1)
    return (mean - z.astype(jnp.float32)).astype(z.dtype), zp[:, -POOL_BUF:]


def swa_prompt(q, k, v, sinks):
    b, s = q.shape[:2]
    nb = s // WINDOW
    qb = q.reshape(b, nb, WINDOW, KV_HEADS, GROUP, HEAD_DIM)
    kb = k.reshape(b, nb, WINDOW, KV_HEADS, HEAD_DIM)
    vb = v.reshape(b, nb, WINDOW, KV_HEADS, HEAD_DIM)
    kk = jnp.concatenate([jnp.concatenate([jnp.zeros_like(kb[:, :1]), kb[:, :-1]], axis=1), kb], axis=2)
    vv = jnp.concatenate([jnp.concatenate([jnp.zeros_like(vb[:, :1]), vb[:, :-1]], axis=1), vb], axis=2)
    qi = jnp.arange(WINDOW)[:, None]
    si = jnp.arange(2 * WINDOW)[None, :]
    diff = qi + WINDOW - si
    band = (diff >= 0) & (diff < WINDOW)
    not_first = jnp.arange(nb)[:, None, None] > 0
    valid = band[None] & (not_first | (si >= WINDOW)[None])
    scores = jnp.einsum('bnqkgd,bnskd->bnkgqs', qb, kk).astype(jnp.float32) * (HEAD_DIM ** -0.5)
    scores = jnp.where(valid[None, :, None, None], scores, NEG_INF)
    sink = jnp.broadcast_to(sinks.astype(jnp.float32).reshape(KV_HEADS, GROUP, 1, 1),
                            scores.shape[:-1] + (1,))
    probs = jax.nn.softmax(jnp.concatenate([scores, sink], axis=-1), axis=-1)[..., :-1]
    out = jnp.einsum('bnkgqs,bnskd->bnqkgd', probs.astype(v.dtype), vv)
    return out.reshape(b, s, ATTN_CH)


def swa_sample(q, k, v, k_buf, v_buf, sinks, pos):
    b, t = q.shape[:2]
    nbuf = k_buf.shape[1]
    kk = jnp.concatenate([k_buf, k], axis=1)
    vv = jnp.concatenate([v_buf, v], axis=1)
    pos_k = jnp.concatenate([pos[0] - nbuf + jnp.arange(nbuf, dtype=pos.dtype), pos])
    diff = pos[:, None] - pos_k[None, :]
    valid = (diff >= 0) & (diff < WINDOW)
    qg = q.reshape(b, t, KV_HEADS, GROUP, HEAD_DIM)
    scores = jnp.einsum('btkgd,bskd->bkgts', qg, kk).astype(jnp.float32) * (HEAD_DIM ** -0.5)
    scores = jnp.where(valid[None, None, None], scores, NEG_INF)
    sink = jnp.broadcast_to(sinks.astype(jnp.float32).reshape(KV_HEADS, GROUP, 1, 1),
                            scores.shape[:-1] + (1,))
    probs = jax.nn.softmax(jnp.concatenate([scores, sink], axis=-1), axis=-1)[..., :-1]
    out = jnp.einsum('bkgts,bskd->btkgd', probs.astype(v.dtype), vv).reshape(b, t, ATTN_CH)
    return out, kk[:, -nbuf:], vv[:, -nbuf:]


def peer_ffn(h, wq, sub_keys, u_tab, v_tab):
    lead = h.shape[:-1]
    tok = h.reshape(-1, D_MODEL)
    n = tok.shape[0]
    tok = jnp.pad(tok, ((0, (-n) % PEER_CHUNK), (0, 0)))

    def one_block(tb):
        c = tb.shape[0]
        q = (tb @ wq).reshape(c, PEER_HEADS, 2, PEER_DK_HALF)
        s = jnp.einsum('chpd,pkd->chpk', q, sub_keys).astype(jnp.float32)
        sv, si = lax.top_k(s, PEER_TOPK)
        cand = (sv[:, :, 0, :, None] + sv[:, :, 1, None, :]).reshape(c, PEER_HEADS, PEER_TOPK * PEER_TOPK)
        cand_idx = (si[:, :, 0, :, None] * N_KEYS + si[:, :, 1, None, :]).reshape(c, PEER_HEADS, PEER_TOPK * PEER_TOPK)
        top_s, top_pos = lax.top_k(cand, PEER_TOPK)
        eidx = jnp.take_along_axis(cand_idx, top_pos, axis=-1).reshape(c, PEER_HEADS * PEER_TOPK)
        gate = jax.nn.softmax(top_s, axis=-1).reshape(c, PEER_HEADS * PEER_TOPK)
        act = jax.nn.gelu(jnp.einsum('ced,cd->ce', u_tab[eidx], tb).astype(jnp.float32), approximate=False)
        coef = (gate * act).astype(tb.dtype)
        return jnp.einsum('ce,ced->cd', coef, v_tab[eidx])

    out = lax.map(one_block, tok.reshape(-1, PEER_CHUNK, D_MODEL))
    return out.reshape(-1, D_MODEL)[:n].reshape(*lead, D_MODEL)


def trunk_layer(x, p, pos, conv_buf, pool_buf, k_buf, v_buf,
                norm_mix, w_in, conv_w, pool_w, pool_scale, q_norm, k_norm, sinks,
                out_norm, w_out, norm_ffn, peer_wq, peer_keys, peer_u, peer_v,
                ple_proj, ple_norm, ple_gate):
    b, t = x.shape[:2]
    h = rmsnorm(x, norm_mix)
    z = h @ w_in
    g_b, g_c, x_c, z_pool, q, k, v = jnp.split(z, SPLIT_POINTS, axis=-1)
    conv_out, conv_state = short_conv(g_c * x_c, conv_buf, conv_w)
    y_a = g_b * conv_out
    pooled, pool_state = multiscale_pool(z_pool, pool_buf, pos)
    y_b = jnp.einsum('btgc,gcd->btgd', pooled.reshape(b, t, POOL_GROUPS, POOL_GROUP_CH), pool_w)
    y_b = y_b.reshape(b, t, POOL_CH) * pool_scale
    q = rope(rmsnorm(q.reshape(b, t, N_HEADS, HEAD_DIM), q_norm), pos)
    k = rope(rmsnorm(k.reshape(b, t, KV_HEADS, HEAD_DIM), k_norm), pos)
    v = v.reshape(b, t, KV_HEADS, HEAD_DIM)
    if k_buf is None:
        y_c = swa_prompt(q, k, v, sinks)
        k_state, v_state = k[:, -WINDOW:], v[:, -WINDOW:]
    else:
        y_c, k_state, v_state = swa_sample(q, k, v, k_buf, v_buf, sinks, pos)
    y = jnp.concatenate([rms_normalize(y_a), rms_normalize(y_b), rms_normalize(y_c)], axis=-1) * out_norm
    x = x + y @ w_out
    x = x + peer_ffn(rmsnorm(x, norm_ffn), peer_wq, peer_keys, peer_u, peer_v)
    gate = jax.nn.sigmoid(rmsnorm(x, ple_norm) @ ple_gate)
    x = x + gate * (p @ ple_proj)
    return x, (k_state, v_state, conv_state, pool_state)


def setup_inputs(seed: int = 0) -> dict:
    key = jax.random.key(seed)
    ks = jax.random.split(key, 26)
    f32 = jnp.float32

    def nrm(k, shape, scale):
        return jax.random.normal(k, shape, f32) * scale

    def gain(k, shape):
        return 1.0 + 0.1 * jax.random.normal(k, shape, f32)

    kv_buf = min(WINDOW, PAST_LEN)
    return {
        'x_prompt': nrm(ks[0], (BATCH, SEQ, D_MODEL), 1.0),
        'x_sample': nrm(ks[1], (DEC_BATCH, DEC_SEQ, D_MODEL), 1.0),
        'cache_k': nrm(ks[2], (DEPTH, DEC_BATCH, kv_buf, KV_HEADS, HEAD_DIM), 1.0),
        'cache_v': nrm(ks[3], (DEPTH, DEC_BATCH, kv_buf, KV_HEADS, HEAD_DIM), 1.0),
        'state_conv': nrm(ks[4], (DEPTH, DEC_BATCH, CONV_W - 1, CONV_CH), 1.0),
        'state_pool': nrm(ks[5], (DEPTH, DEC_BATCH, POOL_BUF, POOL_CH), 1.0),
        'p_prompt': nrm(ks[6], (DEPTH, BATCH, SEQ, PLE_DIM), 1.0),
        'p_sample': nrm(ks[7], (DEPTH, DEC_BATCH, DEC_SEQ, PLE_DIM), 1.0),
        'norm_mix': gain(ks[8], (DEPTH, D_MODEL)),
        'w_in': nrm(ks[9], (DEPTH, D_MODEL, D_IN), D_MODEL ** -0.5),
        'conv_w': nrm(ks[10], (DEPTH, CONV_W, CONV_CH), CONV_W ** -0.5),
        'pool_w': nrm(ks[11], (DEPTH, POOL_GROUPS, POOL_GROUP_CH, POOL_GROUP_CH), POOL_GROUP_CH ** -0.5),
        'pool_scale': gain(ks[12], (DEPTH, POOL_CH)),
        'q_norm': gain(ks[13], (DEPTH, HEAD_DIM)),
        'k_norm': gain(ks[14], (DEPTH, HEAD_DIM)),
        'sinks': nrm(ks[15], (DEPTH, N_HEADS), 0.5),
        'out_norm': gain(ks[16], (DEPTH, D_MIX)),
        'w_out': nrm(ks[17], (DEPTH, D_MIX, D_MODEL), D_MIX ** -0.5),
        'norm_ffn': gain(ks[18], (DEPTH, D_MODEL)),
        'peer_wq': nrm(ks[19], (DEPTH, D_MODEL, PEER_HEADS * PEER_DK), D_MODEL ** -0.5),
        'peer_keys': nrm(ks[20], (DEPTH, 2, N_KEYS, PEER_DK_HALF), PEER_DK_HALF ** -0.5),
        'peer_u': nrm(ks[21], (DEPTH, N_EXPERTS, D_MODEL), D_MODEL ** -0.5),
        'peer_v': nrm(ks[22], (DEPTH, N_EXPERTS, D_MODEL), PEER_HEADS ** -0.5),
        'ple_proj': nrm(ks[23], (DEPTH, PLE_DIM, D_MODEL), PLE_DIM ** -0.5),
        'ple_norm': gain(ks[24], (DEPTH, D_MODEL)),
        'ple_gate': nrm(ks[25], (DEPTH, D_MODEL, D_MODEL), D_MODEL ** -0.5),
    }


def reference(x_prompt, x_sample, cache_k, cache_v, state_conv, state_pool, p_prompt, p_sample,
              norm_mix, w_in, conv_w, pool_w, pool_scale, q_norm, k_norm, sinks, out_norm, w_out,
              norm_ffn, peer_wq, peer_keys, peer_u, peer_v, ple_proj, ple_norm, ple_gate):
    b, s = x_prompt.shape[:2]
    pos_prompt = jnp.arange(s, dtype=jnp.int32)
    pos_sample = PAST_LEN + jnp.arange(x_sample.shape[1], dtype=jnp.int32)
    conv0 = jnp.zeros((b, CONV_W - 1, CONV_CH), x_prompt.dtype)
    pool0 = jnp.zeros((b, POOL_BUF, POOL_CH), x_prompt.dtype)
    xp, xs = x_prompt, x_sample
    kp_l, vp_l, cp_l, pp_l, ks_l, vs_l, cs_l, ps_l = [], [], [], [], [], [], [], []
    for i in range(DEPTH):
        weights = (norm_mix[i], w_in[i], conv_w[i], pool_w[i], pool_scale[i], q_norm[i], k_norm[i],
                   sinks[i], out_norm[i], w_out[i], norm_ffn[i], peer_wq[i], peer_keys[i], peer_u[i],
                   peer_v[i], ple_proj[i], ple_norm[i], ple_gate[i])
        xp, (kp, vp, cp, pp) = trunk_layer(xp, p_prompt[i], pos_prompt, conv0, pool0, None, None, *weights)
        xs, (k2, v2, c2, p2) = trunk_layer(xs, p_sample[i], pos_sample, state_conv[i], state_pool[i],
                                           cache_k[i], cache_v[i], *weights)
        kp_l.append(kp); vp_l.append(vp); cp_l.append(cp); pp_l.append(pp)
        ks_l.append(k2); vs_l.append(v2); cs_l.append(c2); ps_l.append(p2)
    new_k_prompt = jnp.stack(kp_l)
    new_v_prompt = jnp.stack(vp_l)
    new_conv_prompt = jnp.stack(cp_l)
    new_pool_prompt = jnp.stack(pp_l)
    new_k_sample = jnp.stack(ks_l)
    new_v_sample = jnp.stack(vs_l)
    new_conv_sample = jnp.stack(cs_l)
    new_pool_sample = jnp.stack(ps_l)
    return (xp, xs, new_k_prompt, new_v_prompt, new_conv_prompt, new_pool_prompt,
            new_k_sample, new_v_sample, new_conv_sample, new_pool_sample)
```

```python
import functools

import jax
import jax.numpy as jnp
from jax import lax
from jax.experimental import pallas as pl
from jax.experimental.pallas import tpu as pltpu

F32 = jnp.float32
BF16 = jnp.bfloat16

D_MODEL = 1024
DEPTH = 4
PAST_LEN = 8192
CONV_CH = 256
CONV_W = 3
POOL_CH = 256
POOL_GROUP_CH = 64
POOL_BUF = 15
HEAD_DIM = 64
HALF_DIM = HEAD_DIM // 2
N_HEADS = 8
KV_HEADS = 2
ATTN_CH = N_HEADS * HEAD_DIM
KV_CH = KV_HEADS * HEAD_DIM
WINDOW = 128
ROPE_THETA = 10000.0
PEER_HEADS = 8
N_KEYS = 128
PEER_TOPK = 16
PEER_DK_HALF = 128
N_EXPERTS = N_KEYS * N_KEYS
PLE_DIM = 256
EPS = 1e-6
NEG_INF = -1e30
D_IN = 3 * CONV_CH + POOL_CH + ATTN_CH + 2 * KV_CH
OFF_GB, OFF_GC, OFF_XC = 0, CONV_CH, 2 * CONV_CH
OFF_ZP = 3 * CONV_CH
OFF_Q = OFF_ZP + POOL_CH
OFF_K = OFF_Q + ATTN_CH
OFF_V = OFF_K + KV_CH

LANES = 128
VMEM_LIMIT = 52 * 1024 * 1024

MIX_T = 512
SAMPLE_GROUP = 8
PEER_C = 512
PEER_EC = 1024
CAND_COUNTS = tuple(PEER_TOPK // (a + 1) for a in range(PEER_TOPK))
CAND_ROWS = 56


def _rms_rows(x):
    return x * lax.rsqrt(jnp.mean(x * x, axis=-1, keepdims=True) + EPS)


def _seg_mean(x, bd):
    hi = x.astype(BF16)
    r1 = x - hi.astype(F32)
    mid = r1.astype(BF16)
    lo = (r1 - mid.astype(F32)).astype(BF16)
    dot = functools.partial(jnp.dot, preferred_element_type=F32)
    return dot(hi, bd) + dot(mid, bd) + dot(lo, bd)


def _rope(x, cos, sin_signed):
    width = x.shape[1]
    lane = lax.broadcasted_iota(jnp.int32, x.shape, 1)
    first_half = (lane & (HEAD_DIM - 1)) < HALF_DIM
    fwd = pltpu.roll(x, width - HALF_DIM, 1)
    bwd = pltpu.roll(x, HALF_DIM, 1)
    return x * cos + jnp.where(first_half, fwd, bwd) * sin_signed


def _qk_prepare(q, k, cos, sin_signed, qg, kg, bdq):
    cos4 = jnp.concatenate([cos] * (ATTN_CH // LANES), axis=1)
    sin4 = jnp.concatenate([sin_signed] * (ATTN_CH // LANES), axis=1)
    qn = q * lax.rsqrt(_seg_mean(q * q, bdq) + EPS) * qg
    kn = k * lax.rsqrt(_seg_mean(k * k, bdq[:KV_CH, :KV_CH]) + EPS) * kg
    return _rope(qn, cos4, sin4) * (HEAD_DIM ** -0.5), _rope(kn, cos, sin_signed)


def _pool_select(w2, w4, w8, w16):
    lane = lax.broadcasted_iota(jnp.int32, w2.shape, 1)
    return jnp.where(lane < POOL_GROUP_CH, w2,
                     jnp.where(lane < 2 * POOL_GROUP_CH, w4,
                               jnp.where(lane < 3 * POOL_GROUP_CH, w8, w16)))


def _pool_window(shape):
    lane = lax.broadcasted_iota(jnp.int32, shape, 1)
    return jnp.where(lane < POOL_GROUP_CH, 2,
                     jnp.where(lane < 2 * POOL_GROUP_CH, 4,
                               jnp.where(lane < 3 * POOL_GROUP_CH, 8, 16)))


def _softmax_parts(parts, sink):
    m = sink
    for s in parts:
        m = jnp.maximum(m, jnp.max(s, axis=-1, keepdims=True))
    ps = [jnp.exp(s - m) for s in parts]
    denom = jnp.exp(sink - m)
    for p in ps:
        denom = denom + jnp.sum(p, axis=-1, keepdims=True)
    inv = 1.0 / denom
    return [p * inv for p in ps]


def _mix_tail(x, y_a, y_b, y_c, onorm, wout):
    y = jnp.concatenate([_rms_rows(y_a), _rms_rows(y_b), _rms_rows(y_c)], axis=-1) * onorm
    return x + jnp.dot(y.astype(BF16), wout, preferred_element_type=F32)


def _dot_nt(a, b):
    return lax.dot_general(a, b, (((1,), (1,)), ((), ())), preferred_element_type=F32)


def _prompt_mixer_kernel(sinks_ref, x_ref, cos_ref, sin_ref, nmix_ref, win_ref, convw_ref, poolw_ref,
                         pscale_ref, qg_ref, kg_ref, bdq_ref, onorm_ref, wout_ref,
                         x1_ref, kst_ref, vst_ref, cst_ref, pst_ref,
                         kprev, vprev, uext, zext, yc):
    blk = pl.program_id(1)
    t_len = x_ref.shape[0]

    @pl.when(blk == 0)
    def _():
        kprev[...] = jnp.zeros_like(kprev)
        vprev[...] = jnp.zeros_like(vprev)
        uext[0:8, :] = jnp.zeros((8, CONV_CH), F32)
        zext[0:16, :] = jnp.zeros((16, POOL_CH), F32)

    x = x_ref[...]
    h = (_rms_rows(x) * nmix_ref[...]).astype(BF16)
    z = jnp.dot(h, win_ref[...], preferred_element_type=F32)
    g_b = z[:, OFF_GB:OFF_GB + CONV_CH]
    g_c = z[:, OFF_GC:OFF_GC + CONV_CH]
    x_c = z[:, OFF_XC:OFF_XC + CONV_CH]
    zp = z[:, OFF_ZP:OFF_ZP + POOL_CH]
    q = z[:, OFF_Q:OFF_Q + ATTN_CH]
    k = z[:, OFF_K:OFF_K + KV_CH]
    v = z[:, OFF_V:OFF_V + KV_CH]

    u = g_c * x_c
    uext[8:8 + t_len, :] = u
    cw = convw_ref[...]
    conv = uext[6:6 + t_len, :] * cw[0:1, :] + uext[7:7 + t_len, :] * cw[1:2, :] + u * cw[2:3, :]
    y_a = g_b * conv
    cst_ref[...] = uext[8 + t_len - (CONV_W - 1):8 + t_len, :]
    uext[0:8, :] = u[t_len - 8:t_len, :]

    zext[16:16 + t_len, :] = zp
    acc = zp + zext[15:15 + t_len, :]
    w2 = acc
    for sft in (2, 3):
        acc = acc + zext[16 - sft:16 - sft + t_len, :]
    w4 = acc
    for sft in range(4, 8):
        acc = acc + zext[16 - sft:16 - sft + t_len, :]
    w8 = acc
    for sft in range(8, 16):
        acc = acc + zext[16 - sft:16 - sft + t_len, :]
    wsum = _pool_select(w2, w4, w8, acc)
    pos = lax.broadcasted_iota(jnp.int32, (t_len, POOL_CH), 0) + blk * t_len
    cnt = jnp.minimum(_pool_window((t_len, POOL_CH)), pos + 1).astype(F32)
    pooled = wsum / cnt - zp
    y_b = jnp.dot(pooled.astype(BF16), poolw_ref[...], preferred_element_type=F32) * pscale_ref[...]
    pst_ref[...] = zext[16 + t_len - POOL_BUF:16 + t_len, :]
    zext[0:16, :] = zp[t_len - 16:t_len, :]

    qr, kr = _qk_prepare(q, k, cos_ref[...], sin_ref[...], qg_ref[...], kg_ref[...], bdq_ref[...])
    kfull = jnp.concatenate([kprev[...], kr], axis=0)
    vfull = jnp.concatenate([vprev[...], v], axis=0)
    kprev[...] = kr[t_len - WINDOW:t_len, :]
    vprev[...] = v[t_len - WINDOW:t_len, :]
    kst_ref[...] = kr[t_len - WINDOW:t_len, :]
    vst_ref[...] = v[t_len - WINDOW:t_len, :]

    lane = lax.broadcasted_iota(jnp.int32, kfull.shape, 1)
    low = lane < HEAD_DIM
    kswap = pltpu.roll(kfull, HEAD_DIM, 1)
    vswap = pltpu.roll(vfull, HEAD_DIM, 1)
    zero = jnp.zeros_like(kfull)
    kvar = [[jnp.where(low, kfull, zero).astype(BF16), jnp.where(low, zero, kswap).astype(BF16)],
            [jnp.where(low, kswap, zero).astype(BF16), jnp.where(low, zero, kfull).astype(BF16)]]
    vvar = [[jnp.where(low, vfull, zero).astype(BF16), jnp.where(low, zero, vswap).astype(BF16)],
            [jnp.where(low, vswap, zero).astype(BF16), jnp.where(low, zero, vfull).astype(BF16)]]
    qb = qr.astype(BF16)

    qi = lax.broadcasted_iota(jnp.int32, (WINDOW, 2 * WINDOW), 0)
    si = lax.broadcasted_iota(jnp.int32, (WINDOW, 2 * WINDOW), 1)
    dist = si - qi
    dist_first = jnp.where(si >= WINDOW, dist, 0)
    for sb in range(t_len // WINDOW):
        dist_sb = jnp.where(blk == 0, dist_first, dist) if sb == 0 else dist
        valid = jnp.abs(2 * dist_sb - (WINDOW + 1)) <= WINDOW - 1
        rows = slice(sb * WINDOW, (sb + 1) * WINDOW)
        keys = slice(sb * WINDOW, (sb + 2) * WINDOW)
        for pair in range(N_HEADS // 2):
            kv = (2 * pair) // (N_HEADS // KV_HEADS)
            qp = qb[rows, pair * LANES:(pair + 1) * LANES]
            out = None
            for half in range(2):
                sink = sinks_ref[2 * pair + half]
                s = jnp.where(valid, _dot_nt(qp, kvar[kv][half][keys, :]), NEG_INF)
                (p,) = _softmax_parts([s], sink)
                o = jnp.dot(p.astype(BF16), vvar[kv][half][keys, :], preferred_element_type=F32)
                out = o if out is None else out + o
            yc[rows, pair * LANES:(pair + 1) * LANES] = out

    x1_ref[...] = _mix_tail(x, y_a, y_b, yc[...], onorm_ref[...], wout_ref[...])


def _full_spec(shape):
    zeros = (0,) * len(shape)
    return pl.BlockSpec(shape, lambda *_: zeros)


def _prompt_mixer(x, sinks, cos, sin, nmix, win, convw, poolw, pscale, qg, kg, bdq, onorm, wout):
    b, s, _ = x.shape
    t = min(MIX_T, s)
    assert s % t == 0 and t % WINDOW == 0
    grid = (b, s // t)
    row_spec = lambda width: pl.BlockSpec((None, t, width), lambda bi, i: (bi, i, 0))
    tab_spec = pl.BlockSpec((t, LANES), lambda bi, i: (i, 0))
    state_spec = lambda rows, width: pl.BlockSpec((None, rows, width), lambda bi, i: (bi, 0, 0))
    weights = (nmix, win, convw, poolw, pscale, qg, kg, bdq, onorm, wout)
    return pl.pallas_call(
        _prompt_mixer_kernel,
        grid=grid,
        in_specs=[pl.BlockSpec(memory_space=pltpu.SMEM), row_spec(D_MODEL), tab_spec, tab_spec]
        + [_full_spec(w.shape) for w in weights],
        out_specs=[row_spec(D_MODEL), state_spec(WINDOW, KV_CH), state_spec(WINDOW, KV_CH),
                   state_spec(CONV_W - 1, CONV_CH), state_spec(POOL_BUF, POOL_CH)],
        out_shape=[jax.ShapeDtypeStruct((b, s, D_MODEL), F32),
                   jax.ShapeDtypeStruct((b, WINDOW, KV_CH), F32),
                   jax.ShapeDtypeStruct((b, WINDOW, KV_CH), F32),
                   jax.ShapeDtypeStruct((b, CONV_W - 1, CONV_CH), F32),
                   jax.ShapeDtypeStruct((b, POOL_BUF, POOL_CH), F32)],
        scratch_shapes=[pltpu.VMEM((WINDOW, KV_CH), F32), pltpu.VMEM((WINDOW, KV_CH), F32),
                        pltpu.VMEM((8 + t, CONV_CH), F32), pltpu.VMEM((16 + t, POOL_CH), F32),
                        pltpu.VMEM((t, ATTN_CH), F32)],
        compiler_params=pltpu.CompilerParams(dimension_semantics=("arbitrary", "arbitrary"),
                                             vmem_limit_bytes=VMEM_LIMIT),
        name="prompt_mixer",
    )(sinks, x, cos, sin, *weights)


def _sample_mixer_kernel(x_ref, sink_ref, cos_ref, sin_ref, bconv_ref, bpool_ref, ck_ref, cv_ref,
                         nmix_ref, win_ref, convw_ref, poolw_ref, pscale_ref, qg_ref, kg_ref, bdq_ref,
                         onorm_ref, wout_ref,
                         x1_ref, u_ref, zp_ref, knew_ref, vnew_ref,
                         qz, knb, vnb, yc, *, n_new):
    rows_n = x_ref.shape[0]
    grp_rows = SAMPLE_GROUP * n_new
    grp_keys = SAMPLE_GROUP * WINDOW
    n_groups = rows_n // grp_rows

    x = x_ref[...]
    h = (_rms_rows(x) * nmix_ref[...]).astype(BF16)
    z = jnp.dot(h, win_ref[...], preferred_element_type=F32)
    g_b = z[:, OFF_GB:OFF_GB + CONV_CH]
    g_c = z[:, OFF_GC:OFF_GC + CONV_CH]
    x_c = z[:, OFF_XC:OFF_XC + CONV_CH]
    zp = z[:, OFF_ZP:OFF_ZP + POOL_CH]
    q = z[:, OFF_Q:OFF_Q + ATTN_CH]
    k = z[:, OFF_K:OFF_K + KV_CH]
    v = z[:, OFF_V:OFF_V + KV_CH]
    tok = lax.broadcasted_iota(jnp.int32, (rows_n, CONV_CH), 0) % n_new

    def shifted(val, sft, buf_ref):
        if sft >= n_new:
            return buf_ref[sft - 1]
        return jnp.where(tok >= sft, pltpu.roll(val, sft, 0), buf_ref[sft - 1])

    u = g_c * x_c
    u_ref[...] = u
    cw = convw_ref[...]
    conv = shifted(u, 2, bconv_ref) * cw[0:1, :] + shifted(u, 1, bconv_ref) * cw[1:2, :] + u * cw[2:3, :]
    y_a = g_b * conv

    zp_ref[...] = zp
    acc = zp + shifted(zp, 1, bpool_ref)
    w2 = acc
    for sft in (2, 3):
        acc = acc + shifted(zp, sft, bpool_ref)
    w4 = acc
    for sft in range(4, 8):
        acc = acc + shifted(zp, sft, bpool_ref)
    w8 = acc
    for sft in range(8, 16):
        acc = acc + shifted(zp, sft, bpool_ref)
    wsum = _pool_select(w2, w4, w8, acc)
    pooled = wsum / _pool_window((rows_n, POOL_CH)).astype(F32) - zp
    y_b = jnp.dot(pooled.astype(BF16), poolw_ref[...], preferred_element_type=F32) * pscale_ref[...]

    qr, kr = _qk_prepare(q, k, cos_ref[...], sin_ref[...], qg_ref[...], kg_ref[...], bdq_ref[...])
    knew_ref[...] = kr
    vnew_ref[...] = v
    knb[...] = kr.astype(BF16)
    vnb[...] = v.astype(BF16)
    lane = lax.broadcasted_iota(jnp.int32, (rows_n, LANES), 1)
    for hd in range(N_HEADS):
        piece = qr[:, (hd // 2) * LANES:(hd // 2 + 1) * LANES]
        kv = hd // (N_HEADS // KV_HEADS)
        if hd % 2 != kv:
            piece = pltpu.roll(piece, HEAD_DIM, 1)
        keep = (lane < HEAD_DIM) if kv == 0 else (lane >= HEAD_DIM)
        qz[hd * rows_n:(hd + 1) * rows_n, :] = jnp.where(keep, piece, 0.0).astype(BF16)

    n_q = N_HEADS * grp_rows
    r = lax.broadcasted_iota(jnp.int32, (n_q, grp_keys), 0)
    c = lax.broadcasted_iota(jnp.int32, (n_q, grp_keys), 1)
    seq_r = (r % grp_rows) // n_new
    lo = seq_r * WINDOW + (r % n_new) + 1
    hi = (seq_r + 1) * WINDOW
    valid_cache = jnp.where(c >= lo, jnp.where(c < hi, 1.0, 0.0), 0.0)
    r2 = lax.broadcasted_iota(jnp.int32, (n_q, grp_rows), 0)
    c2 = lax.broadcasted_iota(jnp.int32, (n_q, grp_rows), 1)
    same = (r2 % grp_rows) // n_new == c2 // n_new
    valid_new = jnp.where(same, jnp.where(c2 % n_new <= r2 % n_new, 1.0, 0.0), 0.0)
    sink = jnp.max(sink_ref[...], axis=-1, keepdims=True)
    lane_o = lax.broadcasted_iota(jnp.int32, (grp_rows, LANES), 1)

    def group(g, carry):
        rows = pl.ds(pl.multiple_of(g * grp_rows, grp_rows), grp_rows)
        keys = pl.ds(pl.multiple_of(g * grp_keys, grp_keys), grp_keys)
        qg_all = jnp.concatenate(
            [qz[pl.ds(pl.multiple_of(hd * rows_n + g * grp_rows, grp_rows), grp_rows), :]
             for hd in range(N_HEADS)], axis=0)
        s_c = jnp.where(valid_cache > 0.5, _dot_nt(qg_all, ck_ref[keys, :].astype(BF16)), NEG_INF)
        s_n = jnp.where(valid_new > 0.5, _dot_nt(qg_all, knb[rows, :]), NEG_INF)
        p_c, p_n = _softmax_parts([s_c, s_n], sink)
        o = (jnp.dot(p_c.astype(BF16), cv_ref[keys, :].astype(BF16), preferred_element_type=F32)
             + jnp.dot(p_n.astype(BF16), vnb[rows, :], preferred_element_type=F32))
        for pair in range(N_HEADS // 2):
            kv = (2 * pair) // (N_HEADS // KV_HEADS)
            o_even = o[(2 * pair) * grp_rows:(2 * pair + 1) * grp_rows, :]
            o_odd = o[(2 * pair + 1) * grp_rows:(2 * pair + 2) * grp_rows, :]
            if kv == 0:
                o_odd = pltpu.roll(o_odd, HEAD_DIM, 1)
            else:
                o_even = pltpu.roll(o_even, HEAD_DIM, 1)
            yc[rows, pair * LANES:(pair + 1) * LANES] = jnp.where(lane_o < HEAD_DIM, o_even, o_odd)
        return carry

    lax.fori_loop(0, n_groups, group, 0)
    x1_ref[...] = _mix_tail(x, y_a, y_b, yc[...], onorm_ref[...], wout_ref[...])


def _sample_mixer(x, sinkrows, cos, sin, bconv, bpool, ck, cv, nmix, win, convw, poolw, pscale, qg, kg, bdq,
                  onorm, wout, n_new):
    rows_n = x.shape[0]
    assert rows_n % (SAMPLE_GROUP * n_new) == 0
    outs = [jax.ShapeDtypeStruct((rows_n, D_MODEL), F32), jax.ShapeDtypeStruct((rows_n, CONV_CH), F32),
            jax.ShapeDtypeStruct((rows_n, POOL_CH), F32), jax.ShapeDtypeStruct((rows_n, KV_CH), F32),
            jax.ShapeDtypeStruct((rows_n, KV_CH), F32)]
    return pl.pallas_call(
        functools.partial(_sample_mixer_kernel, n_new=n_new),
        out_shape=outs,
        scratch_shapes=[pltpu.VMEM((N_HEADS * rows_n, LANES), BF16), pltpu.VMEM((rows_n, KV_CH), BF16),
                        pltpu.VMEM((rows_n, KV_CH), BF16), pltpu.VMEM((rows_n, ATTN_CH), F32)],
        compiler_params=pltpu.CompilerParams(vmem_limit_bytes=VMEM_LIMIT),
        name="sample_mixer",
    )(x, sinkrows, cos, sin, bconv, bpool, ck, cv, nmix, win, convw, poolw, pscale, qg, kg, bdq, onorm, wout)


def _top16(s, key_iota, vals_ref):
    rank = jnp.full(s.shape, float(PEER_TOPK), F32)
    for rnd in range(PEER_TOPK):
        m = jnp.max(s, axis=0, keepdims=True)
        first = jnp.min(jnp.where(s == m, key_iota, N_KEYS), axis=0, keepdims=True)
        hit = key_iota == first
        rank = jnp.where(hit, float(rnd), rank)
        s = jnp.where(hit, -jnp.inf, s)
        vals_ref[rnd:rnd + 1, :] = m
    return rank


def _router_kernel(x_ref, nffn_ref, wqt_ref, keys_ref,
                   xnt_ref, l1_ref, r2_ref, e1_ref, e2_ref,
                   s_all, sv1, sv2, cand_scr, chosen_scr):
    c_len = x_ref.shape[0]
    xn = _rms_rows(x_ref[...]) * nffn_ref[...]
    xnt = xn.T.astype(BF16)
    xnt_ref[...] = xnt
    qt = jnp.dot(wqt_ref[...], xnt, preferred_element_type=F32)
    for hp in range(2 * PEER_HEADS):
        qhp = qt[hp * PEER_DK_HALF:(hp + 1) * PEER_DK_HALF, :].astype(BF16)
        s_all[hp] = jnp.dot(keys_ref[hp % 2], qhp, preferred_element_type=F32)

    key_iota = lax.broadcasted_iota(jnp.int32, (N_KEYS, LANES), 0)
    pos_rows = [a * PEER_TOPK + b for a, nb in enumerate(CAND_COUNTS) for b in range(nb)]
    row_iota = lax.broadcasted_iota(jnp.int32, (CAND_ROWS, LANES), 0)
    cand_pos = jnp.full((CAND_ROWS, LANES), PEER_TOPK * PEER_TOPK, jnp.int32)
    for row, p in enumerate(pos_rows):
        cand_pos = jnp.where(row_iota == row, p, cand_pos)

    def tile(idx, carry):
        hd = idx // (c_len // LANES)
        lg = idx % (c_len // LANES)
        cols = pl.ds(pl.multiple_of(lg * LANES, LANES), LANES)
        s1 = s_all[2 * hd, :, cols]
        s2 = s_all[2 * hd + 1, :, cols]
        rank1 = _top16(s1, key_iota, sv1)
        rank2 = _top16(s2, key_iota, sv2)
        off = 0
        for a, nb in enumerate(CAND_COUNTS):
            cand_scr[off:off + nb, :] = sv1[a:a + 1, :] + sv2[0:nb, :]
            off += nb
        cand_scr[off:CAND_ROWS, :] = jnp.full((CAND_ROWS - off, LANES), -jnp.inf, F32)
        cand = cand_scr[...]
        work = cand
        chosen = jnp.zeros(cand.shape, F32)
        for _ in range(PEER_TOPK):
            m = jnp.max(work, axis=0, keepdims=True)
            first = jnp.min(jnp.where(work == m, cand_pos, PEER_TOPK * PEER_TOPK), axis=0, keepdims=True)
            hit = cand_pos == first
            chosen = jnp.where(hit, 1.0, chosen)
            work = jnp.where(hit, -jnp.inf, work)
        chosen_scr[...] = chosen
        top = cand[0:1, :]
        z = jnp.sum(jnp.where(chosen > 0.5, jnp.exp(cand - top), 0.0), axis=0, keepdims=True)
        l1 = jnp.zeros(s1.shape, F32)
        off = 0
        for a, nb in enumerate(CAND_COUNTS):
            b_a = jnp.sum(chosen_scr[off:off + nb, :], axis=0, keepdims=True)
            l1 = jnp.where(rank1 == float(a), b_a, l1)
            off += nb
        l1_ref[hd, :, cols] = l1
        r2_ref[hd, :, cols] = rank2
        e1_ref[hd, :, cols] = jnp.exp(s1 - sv1[0:1, :])
        e2_ref[hd, :, cols] = jnp.exp(s2 - sv2[0:1, :]) / z
        return carry

    lax.fori_loop(0, PEER_HEADS * (c_len // LANES), tile, 0)


def _router(x1, nffn, wqt, keys):
    n = x1.shape[0]
    c = min(PEER_C, n)
    assert n % c == 0 and c % LANES == 0
    gate_shape = jax.ShapeDtypeStruct((PEER_HEADS, N_KEYS, n), F32)
    gate_spec = pl.BlockSpec((PEER_HEADS, N_KEYS, c), lambda i: (0, 0, i))
    return pl.pallas_call(
        _router_kernel,
        grid=(n // c,),
        in_specs=[pl.BlockSpec((c, D_MODEL), lambda i: (i, 0)), _full_spec(nffn.shape),
                  _full_spec(wqt.shape), _full_spec(keys.shape)],
        out_specs=[pl.BlockSpec((D_MODEL, c), lambda i: (0, i)), gate_spec, gate_spec, gate_spec, gate_spec],
        out_shape=[jax.ShapeDtypeStruct((D_MODEL, n), BF16), gate_shape, gate_shape, gate_shape, gate_shape],
        scratch_shapes=[pltpu.VMEM((2 * PEER_HEADS, N_KEYS, c), F32), pltpu.VMEM((PEER_TOPK, LANES), F32),
                        pltpu.VMEM((PEER_TOPK, LANES), F32), pltpu.VMEM((CAND_ROWS, LANES), F32),
                        pltpu.VMEM((CAND_ROWS, LANES), F32)],
        compiler_params=pltpu.CompilerParams(dimension_semantics=("arbitrary",),
                                             vmem_limit_bytes=VMEM_LIMIT),
        name="peer_router",
    )(x1, nffn, wqt, keys)


def _expert_kernel(xnt_ref, u_ref, vt_ref, l1_ref, e1_ref, r2_ref, e2_ref, x1_ref, p_ref,
                   plen_ref, pleg_ref, plep_ref, out_ref, acc):
    ec = pl.program_id(1)
    rows_per_step = u_ref.shape[0] // N_KEYS

    @pl.when(ec == 0)
    def _():
        acc[...] = jnp.zeros_like(acc)

    act = jnp.dot(u_ref[...], xnt_ref[...], preferred_element_type=F32)
    tiles = []
    for il in range(rows_per_step):
        coef = None
        for hd in range(PEER_HEADS):
            gate = e2_ref[hd] * e1_ref[hd, il:il + 1, :]
            term = jnp.where(r2_ref[hd] < l1_ref[hd, il:il + 1, :], gate, 0.0)
            coef = term if coef is None else coef + term
        a = act[il * N_KEYS:(il + 1) * N_KEYS, :]
        gelu = 0.5 * a * (1.0 + lax.erf(a * (2.0 ** -0.5)))
        tiles.append((coef * gelu).astype(BF16))
    weighted = jnp.concatenate(tiles, axis=0)
    acc[...] += jnp.dot(vt_ref[...], weighted, preferred_element_type=F32)

    @pl.when(ec == pl.num_programs(1) - 1)
    def _():
        x2 = x1_ref[...] + acc[...].T
        gate_in = (_rms_rows(x2) * plen_ref[...]).astype(BF16)
        gate = jax.nn.sigmoid(jnp.dot(gate_in, pleg_ref[...], preferred_element_type=F32))
        emb = jnp.dot(p_ref[...].astype(BF16), plep_ref[...], preferred_element_type=F32)
        out_ref[...] = x2 + gate * emb


def _experts(xnt, u_tab, vt_tab, l1, r2, e1, e2, x1, p, plen, pleg, plep):
    n = x1.shape[0]
    c = min(PEER_C, n)
    rows_per_step = PEER_EC // N_KEYS
    grid = (n // c, N_EXPERTS // PEER_EC)
    row_gate = pl.BlockSpec((PEER_HEADS, rows_per_step, c), lambda i, j: (0, j, i))
    col_gate = pl.BlockSpec((PEER_HEADS, N_KEYS, c), lambda i, j: (0, 0, i))
    return pl.pallas_call(
        _expert_kernel,
        grid=grid,
        in_specs=[pl.BlockSpec((D_MODEL, c), lambda i, j: (0, i)),
                  pl.BlockSpec((PEER_EC, D_MODEL), lambda i, j: (j, 0)),
                  pl.BlockSpec((D_MODEL, PEER_EC), lambda i, j: (0, j)),
                  row_gate, row_gate, col_gate, col_gate,
                  pl.BlockSpec((c, D_MODEL), lambda i, j: (i, 0)),
                  pl.BlockSpec((c, PLE_DIM), lambda i, j: (i, 0)),
                  _full_spec(plen.shape), _full_spec(pleg.shape), _full_spec(plep.shape)],
        out_specs=pl.BlockSpec((c, D_MODEL), lambda i, j: (i, 0)),
        out_shape=jax.ShapeDtypeStruct((n, D_MODEL), F32),
        scratch_shapes=[pltpu.VMEM((D_MODEL, c), F32)],
        compiler_params=pltpu.CompilerParams(dimension_semantics=("arbitrary", "arbitrary"),
                                             vmem_limit_bytes=VMEM_LIMIT),
        name="peer_experts",
    )(xnt, u_tab, vt_tab, l1, e1, r2, e2, x1, p, plen, pleg, plep)


def _rope_tables(pos):
    inv = ROPE_THETA ** (-jnp.arange(HALF_DIM, dtype=F32) / HALF_DIM)
    ang = pos.astype(F32)[:, None] * inv[None, :]
    cos, sin = jnp.cos(ang), jnp.sin(ang)
    reps = LANES // HEAD_DIM
    return jnp.tile(cos, (1, 2 * reps)), jnp.tile(jnp.concatenate([-sin, sin], axis=1), (1, reps))


def _staged_history(buf, n_new):
    b, nb, ch = buf.shape
    slabs = []
    for sft in range(1, nb + 1):
        rows = [buf[:, nb + t - sft, :] if t < sft else jnp.zeros((b, ch), buf.dtype) for t in range(n_new)]
        slabs.append(jnp.stack(rows, axis=1).reshape(b * n_new, ch))
    return jnp.stack(slabs)


def kernel(x_prompt, x_sample, cache_k, cache_v, state_conv, state_pool, p_prompt, p_sample, norm_mix, w_in,
           conv_w, pool_w, pool_scale, q_norm, k_norm, sinks, out_norm, w_out, norm_ffn, peer_wq, peer_keys,
           peer_u, peer_v, ple_proj, ple_norm, ple_gate):
    b, s, _ = x_prompt.shape
    db, n_new, _ = x_sample.shape
    depth = w_in.shape[0]
    n_cache = cache_k.shape[2]
    assert n_cache == WINDOW and n_new <= WINDOW

    cos_p, sin_p = _rope_tables(jnp.arange(s, dtype=jnp.int32))
    cos_s, sin_s = _rope_tables(jnp.tile(PAST_LEN + jnp.arange(n_new, dtype=jnp.int32), db))
    seg = jnp.arange(ATTN_CH) // HEAD_DIM
    bdq = jnp.where(seg[:, None] == seg[None, :], 1.0 / HEAD_DIM, 0.0).astype(BF16)
    grp = jnp.arange(POOL_CH) // POOL_GROUP_CH

    xp = x_prompt
    xs = x_sample.reshape(db * n_new, D_MODEL)
    outs = [[] for _ in range(8)]
    for i in range(depth):
        nmix = norm_mix[i][None, :]
        win = w_in[i].astype(BF16)
        poolw = jnp.where(grp[:, None] == grp[None, :],
                          jnp.tile(pool_w[i].reshape(POOL_CH, POOL_GROUP_CH), (1, POOL_CH // POOL_GROUP_CH)),
                          0.0).astype(BF16)
        pscale = pool_scale[i][None, :]
        qg = jnp.tile(q_norm[i], N_HEADS)[None, :]
        kg = jnp.tile(k_norm[i], KV_HEADS)[None, :]
        onorm = out_norm[i][None, :]
        wout = w_out[i].astype(BF16)
        mixw = (nmix, win, conv_w[i], poolw, pscale, qg, kg, bdq, onorm, wout)
        nffn = norm_ffn[i][None, :]
        wqt = peer_wq[i].T.astype(BF16)
        keys = peer_keys[i].astype(BF16)
        u_tab = peer_u[i].astype(BF16)
        vt_tab = peer_v[i].T.astype(BF16)
        plew = (ple_norm[i][None, :], ple_gate[i].astype(BF16), ple_proj[i].astype(BF16))

        x1p, kst, vst, cst, pst = _prompt_mixer(xp, sinks[i], cos_p, sin_p, *mixw)
        x1p = x1p.reshape(b * s, D_MODEL)
        xnt, *gates = _router(x1p, nffn, wqt, keys)
        xp = _experts(xnt, u_tab, vt_tab, *gates, x1p, p_prompt[i].reshape(b * s, PLE_DIM), *plew)
        xp = xp.reshape(b, s, D_MODEL)

        sinkrows = jnp.broadcast_to(jnp.repeat(sinks[i], SAMPLE_GROUP * n_new)[:, None],
                                    (N_HEADS * SAMPLE_GROUP * n_new, LANES))
        x1s, u_s, zp_s, knew, vnew = _sample_mixer(
            xs, sinkrows, cos_s, sin_s, _staged_history(state_conv[i], n_new),
            _staged_history(state_pool[i], n_new), cache_k[i].reshape(db * n_cache, KV_CH),
            cache_v[i].reshape(db * n_cache, KV_CH), *mixw, n_new=n_new)
        xnt, *gates = _router(x1s, nffn, wqt, keys)
        xs = _experts(xnt, u_tab, vt_tab, *gates, x1s, p_sample[i].reshape(db * n_new, PLE_DIM), *plew)

        outs[0].append(kst.reshape(b, WINDOW, KV_HEADS, HEAD_DIM))
        outs[1].append(vst.reshape(b, WINDOW, KV_HEADS, HEAD_DIM))
        outs[2].append(cst)
        outs[3].append(pst)
        outs[4].append(jnp.concatenate(
            [cache_k[i][:, n_new:], knew.reshape(db, n_new, KV_HEADS, HEAD_DIM)], axis=1))
        outs[5].append(jnp.concatenate(
            [cache_v[i][:, n_new:], vnew.reshape(db, n_new, KV_HEADS, HEAD_DIM)], axis=1))
        outs[6].append(jnp.concatenate(
            [state_conv[i], u_s.reshape(db, n_new, CONV_CH)], axis=1)[:, -(CONV_W - 1):])
        outs[7].append(jnp.concatenate(
            [state_pool[i], zp_s.reshape(db, n_new, POOL_CH)], axis=1)[:, -POOL_BUF:])
    return (xp, xs.reshape(db, n_new, D_MODEL), *[jnp.stack(o) for o in outs])
```
